```python
import jax, jax.numpy as jnp
from jax import lax
import numpy as np

D_MODEL = 2048
BATCH = 1
SEQ = 8192
DEPTH = 1
DEC_BATCH = 16
DEC_SEQ = 64
PAST_LEN = 2048

CHUNK = 64
POOL_WINDOWS = (2, 4, 8, 16)
POOL_GROUPS = 4
POOL_WIDTH = D_MODEL // 2
POOL_GROUP_WIDTH = POOL_WIDTH // POOL_GROUPS
POOL_LAG = 15
N_RET_HEADS = 8
RET_DK = D_MODEL // 2 // N_RET_HEADS
RET_DV = D_MODEL // N_RET_HEADS
RET_QK_WIDTH = N_RET_HEADS * RET_DK
RET_V_WIDTH = N_RET_HEADS * RET_DV
N_BRANCHES = 2
IN_WIDTH = POOL_WIDTH + 2 * RET_QK_WIDTH + 2 * RET_V_WIDTH + N_BRANCHES * D_MODEL
D_FF = 4 * D_MODEL
ROPE_BASE = 10000.0
EPS = 1e-6

kernel_name = "pool_retention_hybrid_stream_step"


def rms_norm(x, g):
    xf = x.astype(jnp.float32)
    y = xf * lax.rsqrt(jnp.mean(xf * xf, axis=-1, keepdims=True) + EPS)
    return (y * g.astype(jnp.float32)).astype(x.dtype)


def split_in(z):
    sizes = (POOL_WIDTH, RET_QK_WIDTH, RET_QK_WIDTH, RET_V_WIDTH, RET_V_WIDTH, D_MODEL)
    offs = [int(o) for o in np.cumsum(sizes)]
    return jnp.split(z, offs, axis=-1)


def pool_mix(u, left, pos0, w_pool, pool_scale):
    B, T, _ = u.shape
    ext = jnp.concatenate([left, u], axis=1).astype(jnp.float32)
    cs = jnp.concatenate([jnp.zeros((B, 1, POOL_WIDTH), jnp.float32), jnp.cumsum(ext, axis=1)], axis=1)
    end = cs[:, POOL_LAG + 1:]
    pos = pos0 + jnp.arange(T)
    means = []
    for g, w in enumerate(POOL_WINDOWS):
        sl = slice(g * POOL_GROUP_WIDTH, (g + 1) * POOL_GROUP_WIDTH)
        start = cs[:, POOL_LAG + 1 - w: POOL_LAG + 1 - w + T, sl]
        cnt = jnp.minimum(pos + 1, w).astype(jnp.float32)[None, :, None]
        means.append((end[..., sl] - start) / cnt)
    mean = jnp.stack(means, axis=2)
    diff = mean - u.astype(jnp.float32).reshape(B, T, POOL_GROUPS, POOL_GROUP_WIDTH)
    y = jnp.einsum('btgc,gcd->btgd', diff.astype(u.dtype), w_pool).reshape(B, T, POOL_WIDTH)
    new_left = ext[:, -POOL_LAG:].astype(u.dtype)
    return y * pool_scale, new_left


def rotate(x, pos):
    inv = 1.0 / (ROPE_BASE ** jnp.linspace(0.0, 1.0, RET_DK // 2, dtype=jnp.float32))
    ang = pos.astype(jnp.float32)[:, None] * inv[None, :]
    cos = jnp.cos(ang)[None, :, None, :]
    sin = jnp.sin(ang)[None, :, None, :]
    x1, x2 = x[..., 0::2], x[..., 1::2]
    return jnp.stack([x1 * cos - x2 * sin, x1 * sin + x2 * cos], axis=-1).reshape(x.shape)


def retention_chunk(S, q, k, v, log_gamma):
    T = q.shape[1]
    idx = jnp.arange(T, dtype=jnp.float32)
    rel = idx[:, None] - idx[None, :]
    dmask = jnp.where(rel[None] >= 0, jnp.exp(jnp.maximum(rel, 0.0)[None] * log_gamma[:, None, None]), 0.0)
    scores = jnp.einsum('bihd,bjhd->bhij', q, k) * dmask[None]
    inner = jnp.einsum('bhij,bjhe->bihe', scores, v)
    cross = jnp.einsum('bihd,bhde->bihe', q, S) * jnp.exp((idx + 1.0)[:, None] * log_gamma[None, :])[None, :, :, None]
    kdec = k * jnp.exp((T - 1.0 - idx)[:, None] * log_gamma[None, :])[None, :, :, None]
    S_new = jnp.exp(T * log_gamma)[None, :, None, None] * S + jnp.einsum('bjhd,bjhe->bhde', kdec, v)
    return S_new, inner + cross


def retention(q, k, v, S0, log_gamma):
    B, T = q.shape[0], q.shape[1]
    if T <= CHUNK:
        return retention_chunk(S0, q, k, v, log_gamma)
    nc = T // CHUNK
    def blocks(a):
        return jnp.moveaxis(a.reshape(B, nc, CHUNK, a.shape[2], a.shape[3]), 1, 0)
    S_fin, o = lax.scan(lambda S, xs: retention_chunk(S, xs[0], xs[1], xs[2], log_gamma),
                        S0, (blocks(q), blocks(k), blocks(v)))
    o = jnp.moveaxis(o, 0, 1).reshape(B, T, N_RET_HEADS, RET_DV)
    return S_fin, o


def trunk_layer(x, pool_left, ret_state, pos0, g_pre_mix, w_in, w_pool, pool_scale, w_pool_out,
                w_ret_out, w_o, g_post_mix, g_pre_ffn, w_up, w_down, g_post_ffn):
    B, T, _ = x.shape
    h = rms_norm(x, g_pre_mix)
    u, q, k, v, g_r, a_pool, a_ret = split_in(h @ w_in)
    pool_y, new_left = pool_mix(u, pool_left, pos0, w_pool, pool_scale)
    pool_branch = pool_y @ w_pool_out
    pos = pos0 + jnp.arange(T)
    qh = rotate(q.reshape(B, T, N_RET_HEADS, RET_DK).astype(jnp.float32), pos)
    kh = rotate(k.reshape(B, T, N_RET_HEADS, RET_DK).astype(jnp.float32), pos) * (RET_DK ** -0.5)
    vh = v.reshape(B, T, N_RET_HEADS, RET_DV).astype(jnp.float32)
    log_gamma = jnp.log1p(-jnp.exp2(-5.0 - jnp.arange(N_RET_HEADS, dtype=jnp.float32)))
    S_new, o = retention(qh, kh, vh, ret_state.astype(jnp.float32), log_gamma)
    o = o * lax.rsqrt(jnp.mean(o * o, axis=-1, keepdims=True) + EPS)
    ret_branch = (o.reshape(B, T, RET_V_WIDTH).astype(x.dtype) * jax.nn.silu(g_r)) @ w_ret_out
    merged = jax.nn.sigmoid(a_pool) * pool_branch + jax.nn.sigmoid(a_ret) * ret_branch
    x1 = x + rms_norm(merged @ w_o, g_post_mix)
    h2 = rms_norm(x1, g_pre_ffn)
    f = jnp.square(jax.nn.relu(h2 @ w_up)) @ w_down
    y = x1 + rms_norm(f, g_post_ffn)
    return y, new_left, S_new


def setup_inputs(seed: int = 0) -> dict:
    key = jax.random.key(seed)
    ks = jax.random.split(key, 20)
    f32 = jnp.float32
    def nrm(k, shape, scale):
        return jax.random.normal(k, shape, f32) * scale
    def gain(k):
        return 1.0 + 0.05 * jax.random.normal(k, (DEPTH, D_MODEL), f32)
    return {
        "x_prompt": nrm(ks[0], (BATCH, SEQ, D_MODEL), 1.0),
        "x_sample": nrm(ks[1], (DEC_BATCH, DEC_SEQ, D_MODEL), 1.0),
        "cache_pool": nrm(ks[2], (DEPTH, DEC_BATCH, POOL_LAG, POOL_WIDTH), 1.0),
        "state_retention": nrm(ks[3], (DEPTH, DEC_BATCH, N_RET_HEADS, RET_DK, RET_DV), 1.0),
        "g_pre_mix": gain(ks[4]),
        "w_in": nrm(ks[5], (DEPTH, D_MODEL, IN_WIDTH), D_MODEL ** -0.5),
        "w_pool": nrm(ks[6], (DEPTH, POOL_GROUPS, POOL_GROUP_WIDTH, POOL_GROUP_WIDTH), POOL_GROUP_WIDTH ** -0.5),
        "pool_scale": 1.0 + 0.05 * jax.random.normal(ks[7], (DEPTH, POOL_WIDTH), f32),
        "w_pool_out": nrm(ks[8], (DEPTH, POOL_WIDTH, D_MODEL), POOL_WIDTH ** -0.5),
        "w_ret_out": nrm(ks[9], (DEPTH, RET_V_WIDTH, D_MODEL), RET_V_WIDTH ** -0.5),
        "w_o": nrm(ks[10], (DEPTH, D_MODEL, D_MODEL), D_MODEL ** -0.5),
        "g_post_mix": gain(ks[11]),
        "g_pre_ffn": gain(ks[12]),
        "w_up": nrm(ks[13], (DEPTH, D_MODEL, D_FF), D_MODEL ** -0.5),
        "w_down": nrm(ks[14], (DEPTH, D_FF, D_MODEL), D_FF ** -0.5),
        "g_post_ffn": gain(ks[15]),
    }


def reference(x_prompt, x_sample, cache_pool, state_retention, g_pre_mix, w_in, w_pool, pool_scale,
              w_pool_out, w_ret_out, w_o, g_post_mix, g_pre_ffn, w_up, w_down, g_post_ffn):
    xp, xs = x_prompt, x_sample
    pool_p, ret_p, pool_s, ret_s = [], [], [], []
    for l in range(DEPTH):
        params = (g_pre_mix[l], w_in[l], w_pool[l], pool_scale[l], w_pool_out[l], w_ret_out[l], w_o[l],
                  g_post_mix[l], g_pre_ffn[l], w_up[l], w_down[l], g_post_ffn[l])
        left0 = jnp.zeros((xp.shape[0], POOL_LAG, POOL_WIDTH), xp.dtype)
        S0 = jnp.zeros((xp.shape[0], N_RET_HEADS, RET_DK, RET_DV), jnp.float32)
        xp, lp, sp = trunk_layer(xp, left0, S0, 0, *params)
        xs, ls, ss = trunk_layer(xs, cache_pool[l], state_retention[l], PAST_LEN, *params)
        pool_p.append(lp)
        ret_p.append(sp.astype(x_prompt.dtype))
        pool_s.append(ls.astype(cache_pool.dtype))
        ret_s.append(ss.astype(state_retention.dtype))
    return (xp, xs, jnp.stack(pool_p), jnp.stack(ret_p), jnp.stack(pool_s), jnp.stack(ret_s))
```

```python
import functools

import jax
import jax.numpy as jnp
from jax import lax
from jax.experimental import pallas as pl
from jax.experimental.pallas import tpu as pltpu

F32 = jnp.float32
BF16 = jnp.bfloat16

D_MODEL = 2048
POOL_WINDOWS = (2, 4, 8, 16)
POOL_GROUPS = 4
POOL_WIDTH = 1024
POOL_GROUP_WIDTH = 256
POOL_LAG = 15
N_RET_HEADS = 8
RET_DK = 128
RET_DV = 256
RET_QK_WIDTH = 1024
RET_V_WIDTH = 2048
IN_WIDTH = 11264
D_FF = 8192
ROPE_BASE = 10000.0
EPS = 1e-6
PAST_LEN = 2048

Z_V, Z_GR, Z_AP, Z_AR = 0, 1, 2, 3
Z_U, Z_Q, Z_K = 8, 9, 10
Z_ROT = POOL_WIDTH + 2 * RET_QK_WIDTH

LEFT_PAD = 16
VMEM_LIMIT = 56 * 1024 * 1024


def _rms(x, g):
    return x * lax.rsqrt(jnp.mean(x * x, axis=-1, keepdims=True) + EPS) * g


def _params(n_axes):
    return pltpu.CompilerParams(dimension_semantics=("arbitrary",) * n_axes,
                                vmem_limit_bytes=VMEM_LIMIT)


def _in_proj_kernel(x_ref, g_ref, w_ref, o_ref, h_scr):
    @pl.when(pl.program_id(1) == 0)
    def _():
        h_scr[...] = _rms(x_ref[...], g_ref[...]).astype(BF16)

    o_ref[...] = jnp.dot(h_scr[...], w_ref[...], preferred_element_type=F32).astype(o_ref.dtype)


def _in_proj(x, g, w, tm, tn):
    m = x.shape[0]
    return pl.pallas_call(
        _in_proj_kernel,
        grid=(m // tm, IN_WIDTH // tn),
        in_specs=[pl.BlockSpec((tm, D_MODEL), lambda i, j: (i, 0)),
                  pl.BlockSpec((1, D_MODEL), lambda i, j: (0, 0)),
                  pl.BlockSpec((D_MODEL, tn), lambda i, j: (0, j))],
        out_specs=pl.BlockSpec((tm, tn), lambda i, j: (i, j)),
        out_shape=jax.ShapeDtypeStruct((m, IN_WIDTH), BF16),
        scratch_shapes=[pltpu.VMEM((tm, D_MODEL), BF16)],
        compiler_params=_params(2),
        name="in_proj",
    )(x, g, w)


def _seq_kernel(lg_ref, inv_ref, v_ref, gr_ref, u_ref, q_ref, k_ref, *rest, T, pos0, carry):
    if carry:
        diff_ref, ro_ref, left_out_ref, s_out_ref = rest[:4]
        scr = rest[4:]
    else:
        left_ref, s0_ref = rest[:2]
        diff_ref, ro_ref, left_out_ref, s_out_ref = rest[2:6]
        scr = rest[6:]
    s_scr, ext_scr, dmask_scr, rdec_scr, kdec_scr = scr
    step = pl.program_id(0)
    last = pl.num_programs(0) - 1

    @pl.when(step == 0)
    def _():
        row = lax.broadcasted_iota(jnp.int32, (T, T), 0)
        col = lax.broadcasted_iota(jnp.int32, (T, T), 1)
        rel = (row - col).astype(F32)
        idx = lax.broadcasted_iota(jnp.int32, (T, RET_DK), 0).astype(F32)
        for h in range(N_RET_HEADS):
            lg = lg_ref[h]
            dmask_scr[h] = jnp.where(rel >= 0, jnp.exp(jnp.maximum(rel, 0.0) * lg), 0.0)
            rdec_scr[h] = jnp.exp((idx + 1.0) * lg)
            kdec_scr[h] = jnp.exp((T - 1.0 - idx) * lg)
        ext_scr[0:LEFT_PAD, :] = jnp.zeros((LEFT_PAD, POOL_WIDTH), F32)
        if carry:
            s_scr[...] = jnp.zeros(s_scr.shape, F32)

    tile_pos = pos0 + (step * T if carry else 0)
    pos_i = tile_pos + lax.broadcasted_iota(jnp.int32, (T, 1), 0)

    if not carry:
        ext_scr[LEFT_PAD - POOL_LAG:LEFT_PAD, :] = left_ref[0]
    u = u_ref[...].astype(F32)
    ext_scr[LEFT_PAD:LEFT_PAD + T, :] = u
    for g, w in enumerate(POOL_WINDOWS):
        c0, c1 = g * POOL_GROUP_WIDTH, (g + 1) * POOL_GROUP_WIDTH
        acc = u[:, c0:c1]
        for d in range(1, w):
            acc = acc + ext_scr[LEFT_PAD - d:LEFT_PAD - d + T, c0:c1]
        cnt = jnp.minimum(pos_i + 1, w).astype(F32)
        diff_ref[:, c0:c1] = (acc / cnt - u[:, c0:c1]).astype(diff_ref.dtype)
    new_left = ext_scr[T + LEFT_PAD - POOL_LAG:T + LEFT_PAD, :]
    if carry:
        ext_scr[0:LEFT_PAD, :] = ext_scr[T:T + LEFT_PAD, :]

        @pl.when(step == last)
        def _():
            left_out_ref[...] = new_left
    else:
        left_out_ref[0] = new_left

    ang = pos_i.astype(F32) * inv_ref[...]
    cos = jnp.cos(ang)
    sin = jnp.sin(ang)
    even = (lax.broadcasted_iota(jnp.int32, (T, RET_DK), 1) & 1) == 0
    sin_s = jnp.where(even, -sin, sin)

    def rot(xh):
        partner = jnp.where(even, pltpu.roll(xh, RET_DK - 1, 1), pltpu.roll(xh, 1, 1))
        return xh * cos + partner * sin_s

    for h in range(N_RET_HEADS):
        qh = rot(q_ref[:, h * RET_DK:(h + 1) * RET_DK].astype(F32))
        kh = rot(k_ref[:, h * RET_DK:(h + 1) * RET_DK].astype(F32)) * (RET_DK ** -0.5)
        vh = v_ref[:, h * RET_DV:(h + 1) * RET_DV]
        s_prev = s_scr[h] if carry else s0_ref[0, h]
        qb = qh.astype(BF16)
        scores = lax.dot_general(qb, kh.astype(BF16), (((1,), (1,)), ((), ())),
                                 preferred_element_type=F32) * dmask_scr[h]
        inner = jnp.dot(scores.astype(BF16), vh, preferred_element_type=F32)
        cross = jnp.dot((qh * rdec_scr[h]).astype(BF16), s_prev.astype(BF16),
                        preferred_element_type=F32)
        kd = (kh * kdec_scr[h]).astype(BF16)
        decay_t = rdec_scr[h, T - 1:T, 0:1]
        s_new = decay_t * s_prev + lax.dot_general(
            kd, vh, (((0,), (0,)), ((), ())), preferred_element_type=F32)
        if carry:
            s_scr[h] = s_new

            @pl.when(step == last)
            def _():
                s_out_ref[h] = s_new
        else:
            s_out_ref[0, h] = s_new
        o = inner + cross
        o = o * lax.rsqrt(jnp.mean(o * o, axis=-1, keepdims=True) + EPS)
        gate = gr_ref[:, h * RET_DV:(h + 1) * RET_DV].astype(F32)
        ro_ref[:, h * RET_DV:(h + 1) * RET_DV] = (o * (gate * jax.nn.sigmoid(gate))).astype(ro_ref.dtype)


def _seq(z, lg, inv2, left, s0, *, T, pos0, carry):
    m = z.shape[0]
    n = m // T
    zspec = lambda width, blk: pl.BlockSpec((T, width), lambda i, blk=blk: (i, blk))
    in_specs = [pl.BlockSpec(memory_space=pltpu.SMEM),
                pl.BlockSpec((1, RET_DK), lambda i: (0, 0)),
                zspec(RET_V_WIDTH, Z_V), zspec(RET_V_WIDTH, Z_GR),
                zspec(POOL_WIDTH, Z_U), zspec(RET_QK_WIDTH, Z_Q), zspec(RET_QK_WIDTH, Z_K)]
    args = [lg, inv2, z, z, z, z, z]
    if carry:
        left_spec = pl.BlockSpec((POOL_LAG, POOL_WIDTH), lambda i: (0, 0))
        s_spec = pl.BlockSpec((N_RET_HEADS, RET_DK, RET_DV), lambda i: (0, 0, 0))
        left_shape = jax.ShapeDtypeStruct((POOL_LAG, POOL_WIDTH), F32)
        s_shape = jax.ShapeDtypeStruct((N_RET_HEADS, RET_DK, RET_DV), F32)
    else:
        left_spec = pl.BlockSpec((1, POOL_LAG, POOL_WIDTH), lambda i: (i, 0, 0))
        s_spec = pl.BlockSpec((1, N_RET_HEADS, RET_DK, RET_DV), lambda i: (i, 0, 0, 0))
        left_shape = jax.ShapeDtypeStruct((n, POOL_LAG, POOL_WIDTH), F32)
        s_shape = jax.ShapeDtypeStruct((n, N_RET_HEADS, RET_DK, RET_DV), F32)
        in_specs += [left_spec, s_spec]
        args += [left, s0]
    return pl.pallas_call(
        functools.partial(_seq_kernel, T=T, pos0=pos0, carry=carry),
        grid=(n,),
        in_specs=in_specs,
        out_specs=[pl.BlockSpec((T, POOL_WIDTH), lambda i: (i, 0)),
                   pl.BlockSpec((T, RET_V_WIDTH), lambda i: (i, 0)),
                   left_spec, s_spec],
        out_shape=[jax.ShapeDtypeStruct((m, POOL_WIDTH), BF16),
                   jax.ShapeDtypeStruct((m, RET_V_WIDTH), BF16),
                   left_shape, s_shape],
        scratch_shapes=[pltpu.VMEM((N_RET_HEADS, RET_DK, RET_DV), F32),
                        pltpu.VMEM((T + LEFT_PAD, POOL_WIDTH), F32),
                        pltpu.VMEM((N_RET_HEADS, T, T), F32),
                        pltpu.VMEM((N_RET_HEADS, T, RET_DK), F32),
                        pltpu.VMEM((N_RET_HEADS, T, RET_DK), F32)],
        compiler_params=_params(1),
        name="seq_carry" if carry else "seq_batch",
    )(*args)


def _mix_kernel(diff_ref, ro_ref, ap_ref, ar_ref, x_ref, wp_ref, ps_ref, wpo_ref, wro_ref, wo_ref,
                g_ref, o_ref):
    parts = []
    for g in range(POOL_GROUPS):
        d = diff_ref[:, g * POOL_GROUP_WIDTH:(g + 1) * POOL_GROUP_WIDTH]
        parts.append(jnp.dot(d, wp_ref[g], preferred_element_type=F32))
    pool_y = jnp.concatenate(parts, axis=1) * ps_ref[...]
    pool_branch = jnp.dot(pool_y.astype(BF16), wpo_ref[...], preferred_element_type=F32)
    ret_branch = jnp.dot(ro_ref[...], wro_ref[...], preferred_element_type=F32)
    merged = (jax.nn.sigmoid(ap_ref[...].astype(F32)) * pool_branch
              + jax.nn.sigmoid(ar_ref[...].astype(F32)) * ret_branch)
    m = jnp.dot(merged.astype(BF16), wo_ref[...], preferred_element_type=F32)
    o_ref[...] = x_ref[...] + _rms(m, g_ref[...])


def _resident(shape):
    nd = len(shape)
    return pl.BlockSpec(shape, lambda i: (0,) * nd, pipeline_mode=pl.Buffered(1))


def _mix(diff, ro, z, x, w_pool, pool_scale, w_pool_out, w_ret_out, w_o, g_post, tm):
    m = x.shape[0]
    return pl.pallas_call(
        _mix_kernel,
        grid=(m // tm,),
        in_specs=[pl.BlockSpec((tm, POOL_WIDTH), lambda i: (i, 0)),
                  pl.BlockSpec((tm, RET_V_WIDTH), lambda i: (i, 0)),
                  pl.BlockSpec((tm, D_MODEL), lambda i: (i, Z_AP)),
                  pl.BlockSpec((tm, D_MODEL), lambda i: (i, Z_AR)),
                  pl.BlockSpec((tm, D_MODEL), lambda i: (i, 0)),
                  _resident(w_pool.shape), _resident(pool_scale.shape), _resident(w_pool_out.shape),
                  _resident(w_ret_out.shape), _resident(w_o.shape), _resident(g_post.shape)],
        out_specs=pl.BlockSpec((tm, D_MODEL), lambda i: (i, 0)),
        out_shape=jax.ShapeDtypeStruct((m, D_MODEL), F32),
        compiler_params=_params(1),
        name="mix",
    )(diff, ro, z, z, x, w_pool, pool_scale, w_pool_out, w_ret_out, w_o, g_post)


def _ffn_kernel(x1_ref, g1_ref, wu_ref, wd_ref, g2_ref, y_ref, h_scr, acc_scr):
    f = pl.program_id(1)

    @pl.when(f == 0)
    def _():
        h_scr[...] = _rms(x1_ref[...], g1_ref[...]).astype(BF16)

    up = jnp.dot(h_scr[...], wu_ref[...], preferred_element_type=F32)
    act = jnp.square(jnp.maximum(up, 0.0)).astype(BF16)
    part = jnp.dot(act, wd_ref[...], preferred_element_type=F32)

    @pl.when(f == 0)
    def _():
        acc_scr[...] = part

    @pl.when(f > 0)
    def _():
        acc_scr[...] += part

    @pl.when(f == pl.num_programs(1) - 1)
    def _():
        y_ref[...] = x1_ref[...] + _rms(acc_scr[...], g2_ref[...])


def _ffn(x1, g_pre, w_up, w_down, g_post, tm, tf):
    m = x1.shape[0]
    return pl.pallas_call(
        _ffn_kernel,
        grid=(m // tm, D_FF // tf),
        in_specs=[pl.BlockSpec((tm, D_MODEL), lambda i, f: (i, 0)),
                  pl.BlockSpec((1, D_MODEL), lambda i, f: (0, 0)),
                  pl.BlockSpec((D_MODEL, tf), lambda i, f: (0, f)),
                  pl.BlockSpec((tf, D_MODEL), lambda i, f: (f, 0)),
                  pl.BlockSpec((1, D_MODEL), lambda i, f: (0, 0))],
        out_specs=pl.BlockSpec((tm, D_MODEL), lambda i, f: (i, 0)),
        out_shape=jax.ShapeDtypeStruct((m, D_MODEL), F32),
        scratch_shapes=[pltpu.VMEM((tm, D_MODEL), BF16), pltpu.VMEM((tm, D_MODEL), F32)],
        compiler_params=_params(2),
        name="ffn",
    )(x1, g_pre, w_up, w_down, g_post)


def _trunk(x, left, s0, consts, params, *, T, pos0, carry):
    lg, inv2 = consts
    (g_pre_mix, w_in, w_pool, pool_scale, w_pool_out, w_ret_out, w_o, g_post_mix, g_pre_ffn, w_up,
     w_down, g_post_ffn) = params
    z = _in_proj(x, g_pre_mix, w_in, tm=1024, tn=1024)
    diff, ro, new_left, s_new = _seq(z, lg, inv2, left, s0, T=T, pos0=pos0, carry=carry)
    x1 = _mix(diff, ro, z, x, w_pool, pool_scale, w_pool_out, w_ret_out, w_o, g_post_mix, tm=256)
    y = _ffn(x1, g_pre_ffn, w_up, w_down, g_post_ffn, tm=512, tf=1024)
    return y, new_left, s_new


def kernel(x_prompt, x_sample, cache_pool, state_retention, g_pre_mix, w_in, w_pool, pool_scale,
           w_pool_out, w_ret_out, w_o, g_post_mix, g_pre_ffn, w_up, w_down, g_post_ffn):
    batch, seq, _ = x_prompt.shape
    dec_batch, dec_seq, _ = x_sample.shape
    depth = w_in.shape[0]
    assert batch == 1 and depth == 1

    log_gamma = jnp.log1p(-jnp.exp2(-5.0 - jnp.arange(N_RET_HEADS, dtype=F32)))
    inv = 1.0 / (ROPE_BASE ** jnp.linspace(0.0, 1.0, RET_DK // 2, dtype=F32))
    consts = (log_gamma, jnp.repeat(inv, 2).reshape(1, RET_DK))

    l = 0
    w_in_rot = jnp.concatenate([w_in[l][:, Z_ROT:], w_in[l][:, :Z_ROT]], axis=1).astype(BF16)
    row = lambda a: a[l].reshape(1, -1)
    params = (row(g_pre_mix), w_in_rot, w_pool[l].astype(BF16), row(pool_scale),
              w_pool_out[l].astype(BF16), w_ret_out[l].astype(BF16), w_o[l].astype(BF16),
              row(g_post_mix), row(g_pre_ffn), w_up[l].astype(BF16), w_down[l].astype(BF16),
              row(g_post_ffn))

    yp, left_p, s_p = _trunk(x_prompt.reshape(seq, D_MODEL), None, None, consts, params,
                             T=256, pos0=0, carry=True)
    ys, left_s, s_s = _trunk(x_sample.reshape(dec_batch * dec_seq, D_MODEL), cache_pool[l],
                             state_retention[l], consts, params, T=dec_seq, pos0=PAST_LEN, carry=False)

    return (yp.reshape(batch, seq, D_MODEL),
            ys.reshape(dec_batch, dec_seq, D_MODEL),
            left_p.reshape(depth, batch, POOL_LAG, POOL_WIDTH),
            s_p.reshape(depth, batch, N_RET_HEADS, RET_DK, RET_DV),
            left_s.reshape(depth, dec_batch, POOL_LAG, POOL_WIDTH),
            s_s.reshape(depth, dec_batch, N_RET_HEADS, RET_DK, RET_DV))
```

```python
import functools

import jax
import jax.numpy as jnp
from jax import lax
from jax.experimental import pallas as pl
from jax.experimental.pallas import tpu as pltpu

F32 = jnp.float32
BF16 = jnp.bfloat16

D_MODEL = 2048
POOL_WINDOWS = (2, 4, 8, 16)
POOL_GROUPS = 4
POOL_WIDTH = 1024
POOL_GROUP_WIDTH = 256
POOL_LAG = 15
N_RET_HEADS = 8
RET_DK = 128
RET_DV = 256
RET_QK_WIDTH = 1024
RET_V_WIDTH = 2048
IN_WIDTH = 11264
D_FF = 8192
ROPE_BASE = 10000.0
EPS = 1e-6
PAST_LEN = 2048

Z_V, Z_GR, Z_AP, Z_AR = 0, 1, 2, 3
Z_U, Z_Q, Z_K = 8, 9, 10
Z_ROT = POOL_WIDTH + 2 * RET_QK_WIDTH

LEFT_PAD = 16
VMEM_LIMIT = 56 * 1024 * 1024


def _rms(x, g):
    return x * lax.rsqrt(jnp.mean(x * x, axis=-1, keepdims=True) + EPS) * g


def _params(n_axes):
    return pltpu.CompilerParams(dimension_semantics=("arbitrary",) * n_axes,
                                vmem_limit_bytes=VMEM_LIMIT)


def _in_proj_kernel(x_ref, g_ref, w_ref, *rest, cast_w, n_side):
    side_in = rest[:n_side]
    o_ref = rest[n_side]
    outs = rest[n_side + 1:-1]
    h_scr = rest[-1]

    @pl.when(pl.program_id(1) == 0)
    def _():
        h_scr[...] = _rms(x_ref[...], g_ref[...]).astype(BF16)

    w = w_ref[...]
    if cast_w:
        w = w.astype(BF16)
        outs[0][...] = w
        outs = outs[1:]
    o_ref[...] = jnp.dot(h_scr[...], w, preferred_element_type=F32).astype(o_ref.dtype)
    for src, dst in zip(side_in, outs):
        dst[...] = src[...].astype(BF16)


def _in_proj(x, g, w, side, tm, tn):
    m = x.shape[0]
    n_i, n_j = m // tm, IN_WIDTH // tn
    shift = n_j - Z_ROT // tn
    cast_w = w.dtype != BF16
    in_specs = [pl.BlockSpec((tm, D_MODEL), lambda i, j: (i, 0)),
                pl.BlockSpec((1, D_MODEL), lambda i, j: (0, 0)),
                pl.BlockSpec((D_MODEL, tn), lambda i, j: (0, j))]
    out_specs = [pl.BlockSpec((tm, tn), lambda i, j: (i, (j + shift) % n_j))]
    out_shape = [jax.ShapeDtypeStruct((m, IN_WIDTH), BF16)]
    if cast_w:
        assert n_i == 1
        out_specs.append(pl.BlockSpec((D_MODEL, tn), lambda i, j: (0, j)))
        out_shape.append(jax.ShapeDtypeStruct(w.shape, BF16))
    bands_j = min(n_j, 8)
    n_bands = n_i * bands_j
    for a in side:
        rows = a.shape[0] // n_bands
        assert rows * n_bands == a.shape[0] and rows % 16 == 0
        spec = pl.BlockSpec((rows, a.shape[1]),
                            lambda i, j: (i * bands_j + jnp.minimum(j, bands_j - 1), 0))
        in_specs.append(spec)
        out_specs.append(spec)
        out_shape.append(jax.ShapeDtypeStruct(a.shape, BF16))
    return pl.pallas_call(
        functools.partial(_in_proj_kernel, cast_w=cast_w, n_side=len(side)),
        grid=(n_i, n_j),
        in_specs=in_specs,
        out_specs=out_specs,
        out_shape=out_shape,
        scratch_shapes=[pltpu.VMEM((tm, D_MODEL), BF16)],
        compiler_params=_params(2),
        name="in_proj",
    )(x, g, w, *side)


def _seq_kernel(lg_ref, inv_ref, v_ref, gr_ref, u_ref, q_ref, k_ref, *rest, T, pos0, carry):
    if carry:
        diff_ref, ro_ref, left_out_ref, s_out_ref = rest[:4]
        scr = rest[4:]
    else:
        left_ref, s0_ref = rest[:2]
        diff_ref, ro_ref, left_out_ref, s_out_ref = rest[2:6]
        scr = rest[6:]
    s_scr, ext_scr, dmask_scr, rdec_scr, kdec_scr = scr
    step = pl.program_id(0)
    last = pl.num_programs(0) - 1

    @pl.when(step == 0)
    def _():
        row = lax.broadcasted_iota(jnp.int32, (T, T), 0)
        col = lax.broadcasted_iota(jnp.int32, (T, T), 1)
        rel = (row - col).astype(F32)
        idx = lax.broadcasted_iota(jnp.int32, (T, RET_DK), 0).astype(F32)
        for h in range(N_RET_HEADS):
            lg = lg_ref[h]
            dmask_scr[h] = jnp.where(rel >= 0, jnp.exp(jnp.maximum(rel, 0.0) * lg), 0.0)
            rdec_scr[h] = jnp.exp((idx + 1.0) * lg)
            kdec_scr[h] = jnp.exp((T - 1.0 - idx) * lg)
        ext_scr[0:LEFT_PAD, :] = jnp.zeros((LEFT_PAD, POOL_WIDTH), F32)
        if carry:
            s_scr[...] = jnp.zeros(s_scr.shape, F32)

    tile_pos = pos0 + (step * T if carry else 0)
    pos_i = tile_pos + lax.broadcasted_iota(jnp.int32, (T, 1), 0)

    if not carry:
        ext_scr[LEFT_PAD - POOL_LAG:LEFT_PAD, :] = left_ref[0]
    u = u_ref[...].astype(F32)
    ext_scr[LEFT_PAD:LEFT_PAD + T, :] = u
    for g, w in enumerate(POOL_WINDOWS):
        c0, c1 = g * POOL_GROUP_WIDTH, (g + 1) * POOL_GROUP_WIDTH
        acc = u[:, c0:c1]
        for d in range(1, w):
            acc = acc + ext_scr[LEFT_PAD - d:LEFT_PAD - d + T, c0:c1]
        cnt = jnp.minimum(pos_i + 1, w).astype(F32)
        diff_ref[:, c0:c1] = (acc / cnt - u[:, c0:c1]).astype(diff_ref.dtype)
    new_left = ext_scr[T + LEFT_PAD - POOL_LAG:T + LEFT_PAD, :]
    if carry:
        ext_scr[0:LEFT_PAD, :] = ext_scr[T:T + LEFT_PAD, :]

        @pl.when(step == last)
        def _():
            left_out_ref[...] = new_left
    else:
        left_out_ref[0] = new_left

    ang = pos_i.astype(F32) * inv_ref[...]
    cos = jnp.cos(ang)
    sin = jnp.sin(ang)
    even = (lax.broadcasted_iota(jnp.int32, (T, RET_DK), 1) & 1) == 0
    sin_s = jnp.where(even, -sin, sin)

    def rot(xh):
        partner = jnp.where(even, pltpu.roll(xh, RET_DK - 1, 1), pltpu.roll(xh, 1, 1))
        return xh * cos + partner * sin_s

    for h in range(N_RET_HEADS):
        qh = rot(q_ref[:, h * RET_DK:(h + 1) * RET_DK].astype(F32))
        kh = rot(k_ref[:, h * RET_DK:(h + 1) * RET_DK].astype(F32)) * (RET_DK ** -0.5)
        vh = v_ref[:, h * RET_DV:(h + 1) * RET_DV]
        s_prev = s_scr[h] if carry else s0_ref[0, h]
        qb = qh.astype(BF16)
        scores = lax.dot_general(qb, kh.astype(BF16), (((1,), (1,)), ((), ())),
                                 preferred_element_type=F32) * dmask_scr[h]
        inner = jnp.dot(scores.astype(BF16), vh, preferred_element_type=F32)
        cross = jnp.dot((qh * rdec_scr[h]).astype(BF16), s_prev.astype(BF16),
                        preferred_element_type=F32)
        kd = (kh * kdec_scr[h]).astype(BF16)
        decay_t = rdec_scr[h, T - 1:T, 0:1]
        s_new = decay_t * s_prev + lax.dot_general(
            kd, vh, (((0,), (0,)), ((), ())), preferred_element_type=F32)
        if carry:
            s_scr[h] = s_new

            @pl.when(step == last)
            def _():
                s_out_ref[h] = s_new
        else:
            s_out_ref[0, h] = s_new
        o = inner + cross
        o = o * lax.rsqrt(jnp.mean(o * o, axis=-1, keepdims=True) + EPS)
        gate = gr_ref[:, h * RET_DV:(h + 1) * RET_DV].astype(F32)
        ro_ref[:, h * RET_DV:(h + 1) * RET_DV] = (o * (gate * jax.nn.sigmoid(gate))).astype(ro_ref.dtype)


def _seq(z, lg, inv2, left, s0, *, T, pos0, carry):
    m = z.shape[0]
    n = m // T
    zspec = lambda width, blk: pl.BlockSpec((T, width), lambda i, blk=blk: (i, blk))
    in_specs = [pl.BlockSpec(memory_space=pltpu.SMEM),
                pl.BlockSpec((1, RET_DK), lambda i: (0, 0)),
                zspec(RET_V_WIDTH, Z_V), zspec(RET_V_WIDTH, Z_GR),
                zspec(POOL_WIDTH, Z_U), zspec(RET_QK_WIDTH, Z_Q), zspec(RET_QK_WIDTH, Z_K)]
    args = [lg, inv2, z, z, z, z, z]
    if carry:
        left_spec = pl.BlockSpec((POOL_LAG, POOL_WIDTH), lambda i: (0, 0))
        s_spec = pl.BlockSpec((N_RET_HEADS, RET_DK, RET_DV), lambda i: (0, 0, 0))
        left_shape = jax.ShapeDtypeStruct((POOL_LAG, POOL_WIDTH), F32)
        s_shape = jax.ShapeDtypeStruct((N_RET_HEADS, RET_DK, RET_DV), F32)
    else:
        left_spec = pl.BlockSpec((1, POOL_LAG, POOL_WIDTH), lambda i: (i, 0, 0))
        s_spec = pl.BlockSpec((1, N_RET_HEADS, RET_DK, RET_DV), lambda i: (i, 0, 0, 0))
        left_shape = jax.ShapeDtypeStruct((n, POOL_LAG, POOL_WIDTH), F32)
        s_shape = jax.ShapeDtypeStruct((n, N_RET_HEADS, RET_DK, RET_DV), F32)
        in_specs += [left_spec, s_spec]
        args += [left, s0]
    return pl.pallas_call(
        functools.partial(_seq_kernel, T=T, pos0=pos0, carry=carry),
        grid=(n,),
        in_specs=in_specs,
        out_specs=[pl.BlockSpec((T, POOL_WIDTH), lambda i: (i, 0)),
                   pl.BlockSpec((T, RET_V_WIDTH), lambda i: (i, 0)),
                   left_spec, s_spec],
        out_shape=[jax.ShapeDtypeStruct((m, POOL_WIDTH), BF16),
                   jax.ShapeDtypeStruct((m, RET_V_WIDTH), BF16),
                   left_shape, s_shape],
        scratch_shapes=[pltpu.VMEM((N_RET_HEADS, RET_DK, RET_DV), F32),
                        pltpu.VMEM((T + LEFT_PAD, POOL_WIDTH), F32),
                        pltpu.VMEM((N_RET_HEADS, T, T), F32),
                        pltpu.VMEM((N_RET_HEADS, T, RET_DK), F32),
                        pltpu.VMEM((N_RET_HEADS, T, RET_DK), F32)],
        compiler_params=_params(1),
        name="seq_carry" if carry else "seq_batch",
    )(*args)


def _mix_kernel(diff_ref, ro_ref, ap_ref, ar_ref, x_ref, wp_ref, ps_ref, wpo_ref, wro_ref, wo_ref,
                g_ref, o_ref):
    parts = []
    for g in range(POOL_GROUPS):
        d = diff_ref[:, g * POOL_GROUP_WIDTH:(g + 1) * POOL_GROUP_WIDTH]
        parts.append(jnp.dot(d, wp_ref[g], preferred_element_type=F32))
    pool_y = jnp.concatenate(parts, axis=1) * ps_ref[...]
    pool_branch = jnp.dot(pool_y.astype(BF16), wpo_ref[...], preferred_element_type=F32)
    ret_branch = jnp.dot(ro_ref[...], wro_ref[...], preferred_element_type=F32)
    merged = (jax.nn.sigmoid(ap_ref[...].astype(F32)) * pool_branch
              + jax.nn.sigmoid(ar_ref[...].astype(F32)) * ret_branch)
    m = jnp.dot(merged.astype(BF16), wo_ref[...], preferred_element_type=F32)
    o_ref[...] = x_ref[...] + _rms(m, g_ref[...])


def _resident(shape):
    nd = len(shape)
    return pl.BlockSpec(shape, lambda i: (0,) * nd, pipeline_mode=pl.Buffered(1))


def _mix(diff, ro, z, x, w_pool, pool_scale, w_pool_out, w_ret_out, w_o, g_post, tm):
    m = x.shape[0]
    return pl.pallas_call(
        _mix_kernel,
        grid=(m // tm,),
        in_specs=[pl.BlockSpec((tm, POOL_WIDTH), lambda i: (i, 0)),
                  pl.BlockSpec((tm, RET_V_WIDTH), lambda i: (i, 0)),
                  pl.BlockSpec((tm, D_MODEL), lambda i: (i, Z_AP)),
                  pl.BlockSpec((tm, D_MODEL), lambda i: (i, Z_AR)),
                  pl.BlockSpec((tm, D_MODEL), lambda i: (i, 0)),
                  _resident(w_pool.shape), _resident(pool_scale.shape), _resident(w_pool_out.shape),
                  _resident(w_ret_out.shape), _resident(w_o.shape), _resident(g_post.shape)],
        out_specs=pl.BlockSpec((tm, D_MODEL), lambda i: (i, 0)),
        out_shape=jax.ShapeDtypeStruct((m, D_MODEL), F32),
        compiler_params=_params(1),
        name="mix",
    )(diff, ro, z, z, x, w_pool, pool_scale, w_pool_out, w_ret_out, w_o, g_post)


def _ffn_kernel(x1_ref, g1_ref, wu_ref, wd_ref, g2_ref, y_ref, h_scr, acc_scr):
    f = pl.program_id(1)

    @pl.when(f == 0)
    def _():
        h_scr[...] = _rms(x1_ref[...], g1_ref[...]).astype(BF16)
        acc_scr[...] = jnp.zeros(acc_scr.shape, F32)

    up = jnp.dot(h_scr[...], wu_ref[...], preferred_element_type=F32)
    act = jnp.square(jnp.maximum(up, 0.0)).astype(BF16)
    acc_scr[...] += jnp.dot(act, wd_ref[...], preferred_element_type=F32)

    @pl.when(f == pl.num_programs(1) - 1)
    def _():
        y_ref[...] = x1_ref[...] + _rms(acc_scr[...], g2_ref[...])


def _ffn(x1, g_pre, w_up, w_down, g_post, tm, tf):
    m = x1.shape[0]
    return pl.pallas_call(
        _ffn_kernel,
        grid=(m // tm, D_FF // tf),
        in_specs=[pl.BlockSpec((tm, D_MODEL), lambda i, f: (i, 0)),
                  pl.BlockSpec((1, D_MODEL), lambda i, f: (0, 0)),
                  pl.BlockSpec((D_MODEL, tf), lambda i, f: (0, f)),
                  pl.BlockSpec((tf, D_MODEL), lambda i, f: (f, 0)),
                  pl.BlockSpec((1, D_MODEL), lambda i, f: (0, 0))],
        out_specs=pl.BlockSpec((tm, D_MODEL), lambda i, f: (i, 0)),
        out_shape=jax.ShapeDtypeStruct((m, D_MODEL), F32),
        scratch_shapes=[pltpu.VMEM((tm, D_MODEL), BF16), pltpu.VMEM((tm, D_MODEL), F32)],
        compiler_params=_params(2),
        name="ffn",
    )(x1, g_pre, w_up, w_down, g_post)


def _after_in_proj(x, z, left, s0, consts, params, *, T, pos0, carry):
    lg, inv2 = consts
    (w_pool, pool_scale, w_pool_out, w_ret_out, w_o, g_post_mix, g_pre_ffn, w_up, w_down,
     g_post_ffn) = params
    diff, ro, new_left, s_new = _seq(z, lg, inv2, left, s0, T=T, pos0=pos0, carry=carry)
    x1 = _mix(diff, ro, z, x, w_pool, pool_scale, w_pool_out, w_ret_out, w_o, g_post_mix, tm=256)
    y = _ffn(x1, g_pre_ffn, w_up, w_down, g_post_ffn, tm=512, tf=1024)
    return y, new_left, s_new


def kernel(x_prompt, x_sample, cache_pool, state_retention, g_pre_mix, w_in, w_pool, pool_scale,
           w_pool_out, w_ret_out, w_o, g_post_mix, g_pre_ffn, w_up, w_down, g_post_ffn):
    batch, seq, _ = x_prompt.shape
    dec_batch, dec_seq, _ = x_sample.shape
    depth = w_in.shape[0]
    assert batch == 1 and depth == 1

    log_gamma = jnp.log1p(-jnp.exp2(-5.0 - jnp.arange(N_RET_HEADS, dtype=F32)))
    inv = 1.0 / (ROPE_BASE ** jnp.linspace(0.0, 1.0, RET_DK // 2, dtype=F32))
    consts = (log_gamma, jnp.repeat(inv, 2).reshape(1, RET_DK))

    l = 0
    row = lambda a: a[l].reshape(1, -1)
    xp = x_prompt.reshape(seq, D_MODEL)
    xs = x_sample.reshape(dec_batch * dec_seq, D_MODEL)

    zs, w_in_b = _in_proj(xs, row(g_pre_mix), w_in[l], (), tm=1024, tn=512)
    side = (w_up[l], w_down[l], w_pool_out[l], w_ret_out[l], w_o[l],
            w_pool[l].reshape(POOL_WIDTH, POOL_GROUP_WIDTH))
    zp, w_up_b, w_down_b, w_pool_out_b, w_ret_out_b, w_o_b, w_pool_b = _in_proj(
        xp, row(g_pre_mix), w_in_b, side, tm=1024, tn=1024)
    params = (w_pool_b.reshape(POOL_GROUPS, POOL_GROUP_WIDTH, POOL_GROUP_WIDTH), row(pool_scale),
              w_pool_out_b, w_ret_out_b, w_o_b, row(g_post_mix), row(g_pre_ffn), w_up_b, w_down_b,
              row(g_post_ffn))

    yp, left_p, s_p = _after_in_proj(xp, zp, None, None, consts, params, T=256, pos0=0, carry=True)
    ys, left_s, s_s = _after_in_proj(xs, zs, cache_pool[l], state_retention[l], consts, params,
                                     T=dec_seq, pos0=PAST_LEN, carry=False)

    return (yp.reshape(batch, seq, D_MODEL),
            ys.reshape(dec_batch, dec_seq, D_MODEL),
            left_p.reshape(depth, batch, POOL_LAG, POOL_WIDTH),
            s_p.reshape(depth, batch, N_RET_HEADS, RET_DK, RET_DV),
            left_s.reshape(depth, dec_batch, POOL_LAG, POOL_WIDTH),
            s_s.reshape(depth, dec_batch, N_RET_HEADS, RET_DK, RET_DV))
```

```python
import functools

import jax
import jax.numpy as jnp
from jax import lax
from jax.experimental import pallas as pl
from jax.experimental.pallas import tpu as pltpu

F32 = jnp.float32
BF16 = jnp.bfloat16

D_MODEL = 2048
POOL_WINDOWS = (2, 4, 8, 16)
POOL_GROUPS = 4
POOL_WIDTH = 1024
POOL_GROUP_WIDTH = 256
POOL_LAG = 15
N_RET_HEADS = 8
RET_DK = 128
RET_DV = 256
RET_QK_WIDTH = 1024
RET_V_WIDTH = 2048
IN_WIDTH = 11264
D_FF = 8192
ROPE_BASE = 10000.0
EPS = 1e-6
PAST_LEN = 2048

Z_V, Z_GR, Z_AP, Z_AR = 0, 1, 2, 3
Z_U, Z_Q, Z_K = 8, 9, 10
Z_ROT = POOL_WIDTH + 2 * RET_QK_WIDTH

LEFT_PAD = 16
VMEM_LIMIT = 56 * 1024 * 1024


def _rms(x, g):
    return x * lax.rsqrt(jnp.mean(x * x, axis=-1, keepdims=True) + EPS) * g


def _params(n_axes):
    return pltpu.CompilerParams(dimension_semantics=("arbitrary",) * n_axes,
                                vmem_limit_bytes=VMEM_LIMIT)


def _in_proj_kernel(x_ref, g_ref, w_ref, *rest, cast_w, n_side):
    side_in = rest[:n_side]
    o_ref = rest[n_side]
    outs = rest[n_side + 1:-1]
    h_scr = rest[-1]

    @pl.when(pl.program_id(1) == 0)
    def _():
        h_scr[...] = _rms(x_ref[...], g_ref[...]).astype(BF16)

    w = w_ref[...]
    if cast_w:
        w = w.astype(BF16)
        outs[0][...] = w
        outs = outs[1:]
    o_ref[...] = jnp.dot(h_scr[...], w, preferred_element_type=F32).astype(o_ref.dtype)
    for src, dst in zip(side_in, outs):
        dst[...] = src[...].astype(BF16)


def _in_proj(x, g, w, side, tm, tn):
    m = x.shape[0]
    n_i, n_j = m // tm, IN_WIDTH // tn
    shift = n_j - Z_ROT // tn
    cast_w = w.dtype != BF16
    in_specs = [pl.BlockSpec((tm, D_MODEL), lambda i, j: (i, 0)),
                pl.BlockSpec((1, D_MODEL), lambda i, j: (0, 0)),
                pl.BlockSpec((D_MODEL, tn), lambda i, j: (0, j))]
    out_specs = [pl.BlockSpec((tm, tn), lambda i, j: (i, (j + shift) % n_j))]
    out_shape = [jax.ShapeDtypeStruct((m, IN_WIDTH), BF16)]
    if cast_w:
        assert n_i == 1
        out_specs.append(pl.BlockSpec((D_MODEL, tn), lambda i, j: (0, j)))
        out_shape.append(jax.ShapeDtypeStruct(w.shape, BF16))
    bands_j = min(n_j, 8)
    n_bands = n_i * bands_j
    for a in side:
        rows = a.shape[0] // n_bands
        assert rows * n_bands == a.shape[0] and rows % 16 == 0
        spec = pl.BlockSpec((rows, a.shape[1]),
                            lambda i, j: (i * bands_j + jnp.minimum(j, bands_j - 1), 0))
        in_specs.append(spec)
        out_specs.append(spec)
        out_shape.append(jax.ShapeDtypeStruct(a.shape, BF16))
    return pl.pallas_call(
        functools.partial(_in_proj_kernel, cast_w=cast_w, n_side=len(side)),
        grid=(n_i, n_j),
        in_specs=in_specs,
        out_specs=out_specs,
        out_shape=out_shape,
        scratch_shapes=[pltpu.VMEM((tm, D_MODEL), BF16)],
        compiler_params=_params(2),
        name="in_proj",
    )(x, g, w, *side)


def _seq_scratch(T):
    return [pltpu.VMEM((N_RET_HEADS, RET_DK, RET_DV), F32),
            pltpu.VMEM((2 * LEFT_PAD, POOL_WIDTH), F32),
            pltpu.VMEM((N_RET_HEADS, T, T), F32),
            pltpu.VMEM((N_RET_HEADS, T, RET_DK), F32),
            pltpu.VMEM((N_RET_HEADS, T, RET_DK), F32),
            pltpu.VMEM((T, RET_DK), F32),
            pltpu.VMEM((T, RET_DK), F32),
            pltpu.VMEM((POOL_GROUPS, T, T), BF16)]


def _seq_tables(lg_ref, inv_ref, scr, T, zero_state):
    s_scr, prev_scr, dmask_scr, rdec_scr, kdec_scr, cosr_scr, sinr_scr, band_scr = scr
    rel_i = (lax.broadcasted_iota(jnp.int32, (T, T), 0) - lax.broadcasted_iota(jnp.int32, (T, T), 1))
    rel = rel_i.astype(F32)
    idx = lax.broadcasted_iota(jnp.int32, (T, RET_DK), 0).astype(F32)
    for h in range(N_RET_HEADS):
        lg = lg_ref[h]
        dmask_scr[h] = jnp.where(rel >= 0, jnp.exp(jnp.maximum(rel, 0.0) * lg), 0.0)
        rdec_scr[h] = jnp.exp((idx + 1.0) * lg)
        kdec_scr[h] = jnp.exp((T - 1.0 - idx) * lg)
    ang_r = idx * inv_ref[...]
    cosr_scr[...] = jnp.cos(ang_r)
    sinr_scr[...] = jnp.sin(ang_r)
    for g, w in enumerate(POOL_WINDOWS):
        band_scr[g] = jnp.where(rel >= 0, jnp.where(rel < w, 1.0, 0.0), 0.0).astype(BF16)
    prev_scr[...] = jnp.zeros(prev_scr.shape, F32)
    if zero_state:
        s_scr[...] = jnp.zeros(s_scr.shape, F32)


def _pool_diff(u_ref, prev_scr, band_scr, pos_i, diff_dst, T):
    row16 = lax.broadcasted_iota(jnp.int32, (LEFT_PAD, 1), 0)
    for g, w in enumerate(POOL_WINDOWS):
        c0, c1 = g * POOL_GROUP_WIDTH, (g + 1) * POOL_GROUP_WIDTH
        ub = u_ref[:, c0:c1]
        acc = jnp.dot(band_scr[g], ub, preferred_element_type=F32)
        corr = jnp.zeros((LEFT_PAD, POOL_GROUP_WIDTH), F32)
        for d in range(1, w):
            corr = corr + jnp.where(row16 < d, prev_scr[LEFT_PAD - d:2 * LEFT_PAD - d, c0:c1], 0.0)
        cnt = jnp.minimum(pos_i + 1, w).astype(F32)
        uf = ub.astype(F32)
        top = (acc[:LEFT_PAD] + corr) / cnt[:LEFT_PAD] - uf[:LEFT_PAD]
        rest = acc[LEFT_PAD:] / cnt[LEFT_PAD:] - uf[LEFT_PAD:]
        diff_dst[0:LEFT_PAD, c0:c1] = top.astype(diff_dst.dtype)
        diff_dst[LEFT_PAD:T, c0:c1] = rest.astype(diff_dst.dtype)
    prev_scr[0:LEFT_PAD, :] = u_ref[T - LEFT_PAD:T, :].astype(F32)


def _retention(lg_ref, inv_ref, q_ref, k_ref, v_ref, gr_ref, s_in, s_out, ro_dst, scr, tile_pos, T):
    _, _, dmask_scr, rdec_scr, kdec_scr, cosr_scr, sinr_scr, _ = scr
    ang_b = tile_pos * inv_ref[...]
    cb, sb = jnp.cos(ang_b), jnp.sin(ang_b)
    cos = cb * cosr_scr[...] - sb * sinr_scr[...]
    sin = sb * cosr_scr[...] + cb * sinr_scr[...]
    even = (lax.broadcasted_iota(jnp.int32, (T, RET_DK), 1) & 1) == 0
    sin_s = jnp.where(even, -sin, sin)

    def rot(xh):
        partner = jnp.where(even, pltpu.roll(xh, RET_DK - 1, 1), pltpu.roll(xh, 1, 1))
        return xh * cos + partner * sin_s

    for h in range(N_RET_HEADS):
        qh = rot(q_ref[:, h * RET_DK:(h + 1) * RET_DK].astype(F32))
        kh = rot(k_ref[:, h * RET_DK:(h + 1) * RET_DK].astype(F32)) * (RET_DK ** -0.5)
        vh = v_ref[:, h * RET_DV:(h + 1) * RET_DV]
        s_prev = s_in(h)
        scores = lax.dot_general(qh.astype(BF16), kh.astype(BF16), (((1,), (1,)), ((), ())),
                                 preferred_element_type=F32) * dmask_scr[h]
        inner = jnp.dot(scores.astype(BF16), vh, preferred_element_type=F32)
        cross = jnp.dot((qh * rdec_scr[h]).astype(BF16), s_prev.astype(BF16),
                        preferred_element_type=F32)
        kd = (kh * kdec_scr[h]).astype(BF16)
        decay_t = rdec_scr[h, T - 1:T, 0:1]
        s_out(h, decay_t * s_prev + lax.dot_general(kd, vh, (((0,), (0,)), ((), ())),
                                                    preferred_element_type=F32))
        o = inner + cross
        o = o * lax.rsqrt(jnp.mean(o * o, axis=-1, keepdims=True) + EPS)
        gate = gr_ref[:, h * RET_DV:(h + 1) * RET_DV].astype(F32)
        ro_dst[:, h * RET_DV:(h + 1) * RET_DV] = (o * (gate * jax.nn.sigmoid(gate))).astype(ro_dst.dtype)


def _mix_compute(diff_ref, ro_ref, ap_ref, ar_ref, x_ref, wp_ref, ps_ref, wpo_ref, wro_ref, wo_ref,
                 g_ref, o_ref):
    parts = []
    for g in range(POOL_GROUPS):
        d = diff_ref[:, g * POOL_GROUP_WIDTH:(g + 1) * POOL_GROUP_WIDTH]
        parts.append(jnp.dot(d, wp_ref[g], preferred_element_type=F32))
    pool_y = jnp.concatenate(parts, axis=1) * ps_ref[...]
    pool_branch = jnp.dot(pool_y.astype(BF16), wpo_ref[...], preferred_element_type=F32)
    ret_branch = jnp.dot(ro_ref[...], wro_ref[...], preferred_element_type=F32)
    merged = (jax.nn.sigmoid(ap_ref[...].astype(F32)) * pool_branch
              + jax.nn.sigmoid(ar_ref[...].astype(F32)) * ret_branch)
    m = jnp.dot(merged.astype(BF16), wo_ref[...], preferred_element_type=F32)
    o_ref[...] = x_ref[...] + _rms(m, g_ref[...])


def _resident(shape):
    nd = len(shape)
    return pl.BlockSpec(shape, lambda i: (0,) * nd, pipeline_mode=pl.Buffered(1))


def _zspec(T, width, blk):
    return pl.BlockSpec((T, width), lambda i: (i, blk))


def _seqmix_kernel(lg_ref, inv_ref, v_ref, gr_ref, ap_ref, ar_ref, u_ref, q_ref, k_ref, x_ref,
                   wp_ref, ps_ref, wpo_ref, wro_ref, wo_ref, g_ref,
                   o_ref, left_out_ref, s_out_ref, diff_a, ro_a, diff_b, ro_b, *scr, T, n_tiles):
    s_scr, prev_scr = scr[0], scr[1]
    step = pl.program_id(0)

    @pl.when(step == 0)
    def _():
        _seq_tables(lg_ref, inv_ref, scr, T, zero_state=True)
        diff_b[...] = jnp.zeros(diff_b.shape, BF16)
        ro_b[...] = jnp.zeros(ro_b.shape, BF16)

    tile_pos = jnp.minimum(step, n_tiles - 1) * T
    pos_i = tile_pos + lax.broadcasted_iota(jnp.int32, (T, 1), 0)

    def s_store(h, s_new):
        s_scr[h] = s_new

    def body(diff_dst, ro_dst, diff_src, ro_src):
        _mix_compute(diff_src, ro_src, ap_ref, ar_ref, x_ref, wp_ref, ps_ref, wpo_ref, wro_ref,
                     wo_ref, g_ref, o_ref)
        _pool_diff(u_ref, prev_scr, scr[7], pos_i, diff_dst, T)
        _retention(lg_ref, inv_ref, q_ref, k_ref, v_ref, gr_ref, lambda h: s_scr[h], s_store,
                   ro_dst, scr, tile_pos.astype(F32), T)

    @pl.when((step & 1) == 0)
    def _():
        body(diff_a, ro_a, diff_b, ro_b)

    @pl.when((step & 1) == 1)
    def _():
        body(diff_b, ro_b, diff_a, ro_a)

    @pl.when(step == n_tiles - 1)
    def _():
        left_out_ref[...] = prev_scr[LEFT_PAD - POOL_LAG:LEFT_PAD, :]
        s_out_ref[...] = s_scr[...]


def _seqmix(z, x, lg, inv2, w_pool, pool_scale, w_pool_out, w_ret_out, w_o, g_post, T):
    m = x.shape[0]
    n_tiles = m // T
    seq_spec = lambda width, blk: pl.BlockSpec((T, width), lambda i: (jnp.minimum(i, n_tiles - 1), blk))
    mix_spec = lambda blk: pl.BlockSpec((T, D_MODEL), lambda i: (jnp.maximum(i - 1, 0), blk))
    return pl.pallas_call(
        functools.partial(_seqmix_kernel, T=T, n_tiles=n_tiles),
        grid=(n_tiles + 1,),
        in_specs=[pl.BlockSpec(memory_space=pltpu.SMEM),
                  pl.BlockSpec((1, RET_DK), lambda i: (0, 0)),
                  seq_spec(RET_V_WIDTH, Z_V), seq_spec(RET_V_WIDTH, Z_GR),
                  mix_spec(Z_AP), mix_spec(Z_AR),
                  seq_spec(POOL_WIDTH, Z_U), seq_spec(RET_QK_WIDTH, Z_Q), seq_spec(RET_QK_WIDTH, Z_K),
                  mix_spec(0),
                  _resident(w_pool.shape), _resident(pool_scale.shape), _resident(w_pool_out.shape),
                  _resident(w_ret_out.shape), _resident(w_o.shape), _resident(g_post.shape)],
        out_specs=[mix_spec(0),
                   pl.BlockSpec((POOL_LAG, POOL_WIDTH), lambda i: (0, 0)),
                   pl.BlockSpec((N_RET_HEADS, RET_DK, RET_DV), lambda i: (0, 0, 0))],
        out_shape=[jax.ShapeDtypeStruct((m, D_MODEL), F32),
                   jax.ShapeDtypeStruct((POOL_LAG, POOL_WIDTH), F32),
                   jax.ShapeDtypeStruct((N_RET_HEADS, RET_DK, RET_DV), F32)],
        scratch_shapes=[pltpu.VMEM((T, POOL_WIDTH), BF16), pltpu.VMEM((T, RET_V_WIDTH), BF16),
                        pltpu.VMEM((T, POOL_WIDTH), BF16), pltpu.VMEM((T, RET_V_WIDTH), BF16)]
        + _seq_scratch(T),
        compiler_params=_params(1),
        name="seqmix",
    )(lg, inv2, z, z, z, z, z, z, z, x, w_pool, pool_scale, w_pool_out, w_ret_out, w_o, g_post)


def _seq_batch_kernel(lg_ref, inv_ref, v_ref, gr_ref, u_ref, q_ref, k_ref, left_ref, s0_ref,
                      diff_ref, ro_ref, left_out_ref, s_out_ref, *scr, T, pos0):
    prev_scr = scr[1]

    @pl.when(pl.program_id(0) == 0)
    def _():
        _seq_tables(lg_ref, inv_ref, scr, T, zero_state=True)

    prev_scr[LEFT_PAD - POOL_LAG:LEFT_PAD, :] = left_ref[0]
    pos_i = pos0 + lax.broadcasted_iota(jnp.int32, (T, 1), 0)
    _pool_diff(u_ref, prev_scr, scr[7], pos_i, diff_ref, T)
    left_out_ref[0] = prev_scr[LEFT_PAD - POOL_LAG:LEFT_PAD, :]

    def s_store(h, s_new):
        s_out_ref[0, h] = s_new

    _retention(lg_ref, inv_ref, q_ref, k_ref, v_ref, gr_ref, lambda h: s0_ref[0, h], s_store, ro_ref,
               scr, float(pos0), T)


def _seq_batch(z, lg, inv2, left, s0, T, pos0):
    m = z.shape[0]
    n = m // T
    left_spec = pl.BlockSpec((1, POOL_LAG, POOL_WIDTH), lambda i: (i, 0, 0))
    s_spec = pl.BlockSpec((1, N_RET_HEADS, RET_DK, RET_DV), lambda i: (i, 0, 0, 0))
    return pl.pallas_call(
        functools.partial(_seq_batch_kernel, T=T, pos0=pos0),
        grid=(n,),
        in_specs=[pl.BlockSpec(memory_space=pltpu.SMEM),
                  pl.BlockSpec((1, RET_DK), lambda i: (0, 0)),
                  _zspec(T, RET_V_WIDTH, Z_V), _zspec(T, RET_V_WIDTH, Z_GR),
                  _zspec(T, POOL_WIDTH, Z_U), _zspec(T, RET_QK_WIDTH, Z_Q), _zspec(T, RET_QK_WIDTH, Z_K),
                  left_spec, s_spec],
        out_specs=[pl.BlockSpec((T, POOL_WIDTH), lambda i: (i, 0)),
                   pl.BlockSpec((T, RET_V_WIDTH), lambda i: (i, 0)),
                   left_spec, s_spec],
        out_shape=[jax.ShapeDtypeStruct((m, POOL_WIDTH), BF16),
                   jax.ShapeDtypeStruct((m, RET_V_WIDTH), BF16),
                   jax.ShapeDtypeStruct((n, POOL_LAG, POOL_WIDTH), F32),
                   jax.ShapeDtypeStruct((n, N_RET_HEADS, RET_DK, RET_DV), F32)],
        scratch_shapes=_seq_scratch(T),
        compiler_params=_params(1),
        name="seq_batch",
    )(lg, inv2, z, z, z, z, z, left, s0)


def _mix(diff, ro, z, x, w_pool, pool_scale, w_pool_out, w_ret_out, w_o, g_post, tm):
    m = x.shape[0]
    return pl.pallas_call(
        _mix_compute,
        grid=(m // tm,),
        in_specs=[pl.BlockSpec((tm, POOL_WIDTH), lambda i: (i, 0)),
                  pl.BlockSpec((tm, RET_V_WIDTH), lambda i: (i, 0)),
                  _zspec(tm, D_MODEL, Z_AP), _zspec(tm, D_MODEL, Z_AR),
                  pl.BlockSpec((tm, D_MODEL), lambda i: (i, 0)),
                  _resident(w_pool.shape), _resident(pool_scale.shape), _resident(w_pool_out.shape),
                  _resident(w_ret_out.shape), _resident(w_o.shape), _resident(g_post.shape)],
        out_specs=pl.BlockSpec((tm, D_MODEL), lambda i: (i, 0)),
        out_shape=jax.ShapeDtypeStruct((m, D_MODEL), F32),
        compiler_params=_params(1),
        name="mix",
    )(diff, ro, z, z, x, w_pool, pool_scale, w_pool_out, w_ret_out, w_o, g_post)


def _ffn_kernel(x1_ref, g1_ref, wu_ref, wd_ref, g2_ref, y_ref, h_scr, acc_scr):
    f = pl.program_id(1)

    @pl.when(f == 0)
    def _():
        h_scr[...] = _rms(x1_ref[...], g1_ref[...]).astype(BF16)
        acc_scr[...] = jnp.zeros(acc_scr.shape, F32)

    up = jnp.dot(h_scr[...], wu_ref[...], preferred_element_type=F32)
    act = jnp.square(jnp.maximum(up, 0.0)).astype(BF16)
    acc_scr[...] += jnp.dot(act, wd_ref[...], preferred_element_type=F32)

    @pl.when(f == pl.num_programs(1) - 1)
    def _():
        y_ref[...] = x1_ref[...] + _rms(acc_scr[...], g2_ref[...])


def _ffn(x1, g_pre, w_up, w_down, g_post, tm, tf):
    m = x1.shape[0]
    return pl.pallas_call(
        _ffn_kernel,
        grid=(m // tm, D_FF // tf),
        in_specs=[pl.BlockSpec((tm, D_MODEL), lambda i, f: (i, 0)),
                  pl.BlockSpec((1, D_MODEL), lambda i, f: (0, 0)),
                  pl.BlockSpec((D_MODEL, tf), lambda i, f: (0, f)),
                  pl.BlockSpec((tf, D_MODEL), lambda i, f: (f, 0)),
                  pl.BlockSpec((1, D_MODEL), lambda i, f: (0, 0))],
        out_specs=pl.BlockSpec((tm, D_MODEL), lambda i, f: (i, 0)),
        out_shape=jax.ShapeDtypeStruct((m, D_MODEL), F32),
        scratch_shapes=[pltpu.VMEM((tm, D_MODEL), BF16), pltpu.VMEM((tm, D_MODEL), F32)],
        compiler_params=_params(2),
        name="ffn",
    )(x1, g_pre, w_up, w_down, g_post)


def kernel(x_prompt, x_sample, cache_pool, state_retention, g_pre_mix, w_in, w_pool, pool_scale,
           w_pool_out, w_ret_out, w_o, g_post_mix, g_pre_ffn, w_up, w_down, g_post_ffn):
    batch, seq, _ = x_prompt.shape
    dec_batch, dec_seq, _ = x_sample.shape
    depth = w_in.shape[0]
    assert batch == 1 and depth == 1

    lg = jnp.log1p(-jnp.exp2(-5.0 - jnp.arange(N_RET_HEADS, dtype=F32)))
    inv = 1.0 / (ROPE_BASE ** jnp.linspace(0.0, 1.0, RET_DK // 2, dtype=F32))
    inv2 = jnp.repeat(inv, 2).reshape(1, RET_DK)

    l = 0
    row = lambda a: a[l].reshape(1, -1)
    xp = x_prompt.reshape(seq, D_MODEL)
    xs = x_sample.reshape(dec_batch * dec_seq, D_MODEL)

    zs, w_in_b = _in_proj(xs, row(g_pre_mix), w_in[l], (), tm=1024, tn=512)
    side = (w_up[l], w_down[l], w_pool_out[l], w_ret_out[l], w_o[l],
            w_pool[l].reshape(POOL_WIDTH, POOL_GROUP_WIDTH))
    zp, w_up_b, w_down_b, w_pool_out_b, w_ret_out_b, w_o_b, w_pool_b = _in_proj(
        xp, row(g_pre_mix), w_in_b, side, tm=1024, tn=1024)
    mix_w = (w_pool_b.reshape(POOL_GROUPS, POOL_GROUP_WIDTH, POOL_GROUP_WIDTH), row(pool_scale),
             w_pool_out_b, w_ret_out_b, w_o_b, row(g_post_mix))
    ffn_w = (row(g_pre_ffn), w_up_b, w_down_b, row(g_post_ffn))

    x1p, left_p, s_p = _seqmix(zp, xp, lg, inv2, *mix_w, T=256)
    yp = _ffn(x1p, *ffn_w, tm=512, tf=1024)

    diff_s, ro_s, left_s, s_s = _seq_batch(zs, lg, inv2, cache_pool[l], state_retention[l],
                                           T=dec_seq, pos0=PAST_LEN)
    x1s = _mix(diff_s, ro_s, zs, xs, *mix_w, tm=256)
    ys = _ffn(x1s, *ffn_w, tm=512, tf=1024)

    return (yp.reshape(batch, seq, D_MODEL),
            ys.reshape(dec_batch, dec_seq, D_MODEL),
            left_p.reshape(depth, batch, POOL_LAG, POOL_WIDTH),
            s_p.reshape(depth, batch, N_RET_HEADS, RET_DK, RET_DV),
            left_s.reshape(depth, dec_batch, POOL_LAG, POOL_WIDTH),
            s_s.reshape(depth, dec_batch, N_RET_HEADS, RET_DK, RET_DV))
```
